```python
import math
import jax, jax.numpy as jnp
from jax import lax
import numpy as np

D_MODEL = 2048
BATCH = 4
SEQ = 2048
DEPTH = 4
DEC_BATCH = 8
DEC_SEQ = 4
PAST_LEN = 16384
PAGE_SIZE = 128

N_META = 16
ATTN_WIDTH = D_MODEL // 2
SSM_WIDTH = D_MODEL - ATTN_WIDTH
HEAD_DIM = 128
N_HEADS = ATTN_WIDTH // HEAD_DIM
SSM_GROUP = 16
N_SSM_GROUPS = SSM_WIDTH // SSM_GROUP
SSM_STATE = 64
IN_WIDTH = 3 * ATTN_WIDTH + N_HEADS + SSM_WIDTH
D_FF = -(-8 * D_MODEL // (3 * 256)) * 256
Q_BLOCK = 128
EPS = 1e-6
NEG_INF = -1e30
DT_MIN = 1e-3
DT_MAX = 1e-1
FORGET_BIAS_LO = 4.0
FORGET_BIAS_HI = 10.0

kernel_name = 'hymba_fox_s5_decoder_step'


def rmsnorm(x, g):
    xf = x.astype(jnp.float32)
    y = xf * lax.rsqrt(jnp.mean(xf * xf, axis=-1, keepdims=True) + EPS)
    return (y * g.astype(jnp.float32)).astype(x.dtype)


def mixer_inputs(h, w_in, b_f):
    p = h @ w_in
    q, k, v, f, u = jnp.split(p, [ATTN_WIDTH, 2 * ATTN_WIDTH, 3 * ATTN_WIDTH, 3 * ATTN_WIDTH + N_HEADS], axis=-1)
    heads = lambda t: t.reshape(t.shape[0], t.shape[1], N_HEADS, HEAD_DIM)
    logf = jax.nn.log_sigmoid(f.astype(jnp.float32) + b_f.astype(jnp.float32))
    return heads(q), heads(k), heads(v), logf, u


def fox_attend(q, k, v, cq, ck, q_pos, k_pos):
    s = jnp.einsum('bqhd,bkhd->bhqk', q, k).astype(jnp.float32) / math.sqrt(HEAD_DIM)
    s = s + jnp.swapaxes(cq, 1, 2)[..., :, None] - jnp.swapaxes(ck, 1, 2)[..., None, :]
    mask = k_pos[None, :] <= q_pos[:, None]
    s = jnp.where(mask, s, NEG_INF)
    p = jax.nn.softmax(s, axis=-1)
    return jnp.einsum('bhqk,bkhd->bqhd', p.astype(v.dtype), v)


def fox_prompt(q, k, v, logf):
    b, t = q.shape[0], q.shape[1]
    c = jnp.cumsum(logf, axis=1)
    pos = jnp.arange(t)
    meta_out = fox_attend(q[:, :N_META], k, v, c[:, :N_META], c, pos[:N_META], pos)
    nb = (t - N_META) // Q_BLOCK
    qb = jnp.swapaxes(q[:, N_META:].reshape(b, nb, Q_BLOCK, N_HEADS, HEAD_DIM), 0, 1)
    cb = jnp.swapaxes(c[:, N_META:].reshape(b, nb, Q_BLOCK, N_HEADS), 0, 1)
    pb = pos[N_META:].reshape(nb, Q_BLOCK)
    out = lax.map(lambda a: fox_attend(a[0], k, v, a[1], c, a[2], pos), (qb, cb, pb))
    out = jnp.swapaxes(out, 0, 1).reshape(b, t - N_META, N_HEADS, HEAD_DIM)
    return jnp.concatenate([meta_out, out], axis=1)


def fox_sample(q, k_new, v_new, logf_new, k_past, v_past, logf_past):
    past_len = k_past.shape[1]
    c_past = logf_past - lax.cumsum(logf_past, axis=1, reverse=True)
    c_new = jnp.cumsum(logf_new, axis=1)
    k = jnp.concatenate([k_past, k_new], axis=1)
    v = jnp.concatenate([v_past, v_new], axis=1)
    c = jnp.concatenate([c_past, c_new], axis=1)
    k_pos = jnp.arange(past_len + q.shape[1])
    q_pos = past_len + jnp.arange(q.shape[1])
    return fox_attend(q, k, v, c_new, c, q_pos, k_pos)


def _ssm_combine(e1, e2):
    a1, b1 = e1
    a2, b2 = e2
    return a1 * a2, a2 * b1 + b2


def s5_mixer(u, lam_re, lam_im, log_dt, b_re, b_im, c_re, c_im, d_skip, w_glu, h0=None):
    f32 = jnp.float32
    bn, t = u.shape[0], u.shape[1]
    uf = u.astype(f32).reshape(bn, t, N_SSM_GROUPS, SSM_GROUP)
    lam = lax.complex(lam_re.astype(f32), lam_im.astype(f32))
    dt = jnp.exp(log_dt.astype(f32))[:, None]
    lam_bar = jnp.exp(lam * dt)
    b_mat = lax.complex(b_re.astype(f32), b_im.astype(f32))
    b_bar = ((lam_bar - 1.0) / lam)[..., None] * b_mat
    bu = jnp.einsum('gnc,btgc->btgn', b_bar, uf.astype(jnp.complex64))
    if h0 is not None:
        bu = bu.at[:, 0].add(lam_bar * h0)
    a = jnp.broadcast_to(lam_bar, bu.shape)
    _, h = lax.associative_scan(_ssm_combine, (a, bu), axis=1)
    c_mat = lax.complex(c_re.astype(f32), c_im.astype(f32))
    y = jnp.einsum('gcn,btgn->btgc', c_mat, h).real + d_skip.astype(f32) * uf
    y = y.reshape(bn, t, SSM_WIDTH)
    z = jax.nn.gelu(y)
    out = z * jax.nn.sigmoid(z @ w_glu.astype(f32))
    return out.astype(u.dtype), h[:, -1]


def mix_out(attn, ssm, g_attn, g_ssm, w_out):
    a = rmsnorm(attn.reshape(attn.shape[0], attn.shape[1], ATTN_WIDTH), g_attn)
    s = rmsnorm(ssm, g_ssm)
    return jnp.concatenate([a, s.astype(a.dtype)], axis=-1) @ w_out


def swiglu(h, w_gate, w_up, w_down):
    return (jax.nn.silu(h @ w_gate) * (h @ w_up)) @ w_down


def setup_inputs(seed: int = 0) -> dict:
    key = jax.random.key(seed)
    ks = jax.random.split(key, 32)
    f32 = jnp.float32
    nrm = lambda i, shape, scale: scale * jax.random.normal(ks[i], shape, f32)
    n_pages = PAST_LEN // PAGE_SIZE
    n_used = DEC_BATCH * n_pages
    n_pool = n_used + max(1, n_used // 4)
    page_table = jax.random.permutation(ks[7], n_pool)[:n_used].reshape(DEC_BATCH, n_pages).astype(jnp.int32)
    G, N = N_SSM_GROUPS, SSM_STATE
    head_bias = jnp.linspace(FORGET_BIAS_LO, FORGET_BIAS_HI, N_HEADS, dtype=f32)
    return {
        'x_prompt': nrm(0, (BATCH, SEQ, D_MODEL), 1.0),
        'x_sample': nrm(1, (DEC_BATCH, DEC_SEQ, D_MODEL), 1.0),
        'cache_k': nrm(2, (DEPTH, n_pool, PAGE_SIZE, N_HEADS, HEAD_DIM), 1.0),
        'cache_v': nrm(3, (DEPTH, n_pool, PAGE_SIZE, N_HEADS, HEAD_DIM), 1.0),
        'cache_logf': jax.nn.log_sigmoid(head_bias + nrm(4, (DEPTH, n_pool, PAGE_SIZE, N_HEADS), 0.5)),
        'state_ssm_re': nrm(5, (DEPTH, DEC_BATCH, G, N), 0.1),
        'state_ssm_im': nrm(6, (DEPTH, DEC_BATCH, G, N), 0.1),
        'page_table': page_table,
        'meta_tokens': nrm(8, (N_META, D_MODEL), 1.0),
        'norm_mix': 1.0 + nrm(9, (DEPTH, D_MODEL), 0.05),
        'w_in': nrm(10, (DEPTH, D_MODEL, IN_WIDTH), D_MODEL ** -0.5),
        'b_forget': head_bias + nrm(11, (DEPTH, N_HEADS), 0.5),
        'ssm_lambda_re': -0.5 + nrm(12, (DEPTH, G, N), 0.01),
        'ssm_lambda_im': np.pi * jnp.arange(N, dtype=f32) + nrm(13, (DEPTH, G, N), 0.01),
        'ssm_log_dt': jax.random.uniform(ks[14], (DEPTH, G), f32, math.log(DT_MIN), math.log(DT_MAX)),
        'ssm_b_re': nrm(15, (DEPTH, G, N, SSM_GROUP), (2 * SSM_GROUP) ** -0.5),
        'ssm_b_im': nrm(16, (DEPTH, G, N, SSM_GROUP), (2 * SSM_GROUP) ** -0.5),
        'ssm_c_re': nrm(17, (DEPTH, G, SSM_GROUP, N), SSM_STATE ** -0.5),
        'ssm_c_im': nrm(18, (DEPTH, G, SSM_GROUP, N), SSM_STATE ** -0.5),
        'ssm_d': nrm(19, (DEPTH, G, SSM_GROUP), 1.0),
        'ssm_w_glu': nrm(20, (DEPTH, SSM_WIDTH, SSM_WIDTH), SSM_WIDTH ** -0.5),
        'norm_attn_out': 1.0 + nrm(21, (DEPTH, ATTN_WIDTH), 0.05),
        'norm_ssm_out': 1.0 + nrm(22, (DEPTH, SSM_WIDTH), 0.05),
        'w_out': nrm(23, (DEPTH, D_MODEL, D_MODEL), D_MODEL ** -0.5),
        'norm_ffn': 1.0 + nrm(24, (DEPTH, D_MODEL), 0.05),
        'w_ffn_gate': nrm(25, (DEPTH, D_MODEL, D_FF), D_MODEL ** -0.5),
        'w_ffn_up': nrm(26, (DEPTH, D_MODEL, D_FF), D_MODEL ** -0.5),
        'w_ffn_down': nrm(27, (DEPTH, D_FF, D_MODEL), D_FF ** -0.5),
        'norm_final': 1.0 + nrm(28, (D_MODEL,), 0.05),
    }


def reference(x_prompt, x_sample, cache_k, cache_v, cache_logf, state_ssm_re, state_ssm_im, page_table,
              meta_tokens, norm_mix, w_in, b_forget, ssm_lambda_re, ssm_lambda_im, ssm_log_dt,
              ssm_b_re, ssm_b_im, ssm_c_re, ssm_c_im, ssm_d, ssm_w_glu, norm_attn_out, norm_ssm_out,
              w_out, norm_ffn, w_ffn_gate, w_ffn_up, w_ffn_down, norm_final):
    f32 = jnp.float32
    n_prompt = x_prompt.shape[0]
    n_seq = x_sample.shape[0]
    meta = jnp.broadcast_to(meta_tokens.astype(x_prompt.dtype)[None], (n_prompt, N_META, D_MODEL))
    xp = jnp.concatenate([meta, x_prompt], axis=1)
    xs = x_sample
    kp, vp, fp, srp, sip = [], [], [], [], []
    kd, vd, fd, srd, sid = [], [], [], [], []
    for l in range(DEPTH):
        ssm_p = (ssm_lambda_re[l], ssm_lambda_im[l], ssm_log_dt[l], ssm_b_re[l], ssm_b_im[l],
                 ssm_c_re[l], ssm_c_im[l], ssm_d[l], ssm_w_glu[l])
        q, k, v, logf, u = mixer_inputs(rmsnorm(xp, norm_mix[l]), w_in[l], b_forget[l])
        attn = fox_prompt(q, k, v, logf)
        ssm, h_last = s5_mixer(u, *ssm_p)
        xp = xp + mix_out(attn, ssm, norm_attn_out[l], norm_ssm_out[l], w_out[l])
        xp = xp + swiglu(rmsnorm(xp, norm_ffn[l]), w_ffn_gate[l], w_ffn_up[l], w_ffn_down[l])
        kp.append(k); vp.append(v); fp.append(logf)
        srp.append(jnp.real(h_last)); sip.append(jnp.imag(h_last))
        q, k, v, logf, u = mixer_inputs(rmsnorm(xs, norm_mix[l]), w_in[l], b_forget[l])
        k_past = cache_k[l][page_table].reshape(n_seq, -1, N_HEADS, HEAD_DIM)
        v_past = cache_v[l][page_table].reshape(n_seq, -1, N_HEADS, HEAD_DIM)
        f_past = cache_logf[l][page_table].reshape(n_seq, -1, N_HEADS).astype(f32)
        attn = fox_sample(q, k, v, logf, k_past, v_past, f_past)
        h0 = lax.complex(state_ssm_re[l].astype(f32), state_ssm_im[l].astype(f32))
        ssm, h_last = s5_mixer(u, *ssm_p, h0=h0)
        xs = xs + mix_out(attn, ssm, norm_attn_out[l], norm_ssm_out[l], w_out[l])
        xs = xs + swiglu(rmsnorm(xs, norm_ffn[l]), w_ffn_gate[l], w_ffn_up[l], w_ffn_down[l])
        kd.append(k); vd.append(v); fd.append(logf)
        srd.append(jnp.real(h_last)); sid.append(jnp.imag(h_last))
    y_prompt = rmsnorm(xp[:, N_META:], norm_final)
    y_sample = rmsnorm(xs, norm_final)
    return (y_prompt, y_sample,
            jnp.stack(kp), jnp.stack(vp), jnp.stack(fp), jnp.stack(srp), jnp.stack(sip),
            jnp.stack(kd), jnp.stack(vd), jnp.stack(fd), jnp.stack(srd), jnp.stack(sid))
```

```python
import functools
import math

import jax
import jax.numpy as jnp
from jax import lax
from jax.experimental import pallas as pl
from jax.experimental.pallas import tpu as pltpu

HEAD_DIM = 128
N_HEADS = 8
ATTN_WIDTH = N_HEADS * HEAD_DIM
N_META = 16
SSM_GROUP = 16
SSM_STATE = 64
EPS = 1e-6
NEG_INF = -1e30

V7X_VMEM_LIMIT_BYTES = 60000 * 1024
SUBLANES = 8
LANES = 128

_F32 = jnp.float32
_BF16 = jnp.bfloat16


def _params(semantics, vmem=V7X_VMEM_LIMIT_BYTES):
    return pltpu.CompilerParams(dimension_semantics=semantics, vmem_limit_bytes=vmem)


def _resident(shape):
    zeros = (0,) * len(shape)
    return pl.BlockSpec(shape, lambda *_: zeros, pipeline_mode=pl.Buffered(1))


def _largest_tile(n, limit, multiple):
    best = None
    for t in range(multiple, min(n, limit) + 1, multiple):
        if n % t == 0:
            best = t
    return best if best is not None else n


def _rms(x, g):
    return x * lax.rsqrt(jnp.mean(x * x, axis=-1, keepdims=True) + EPS) * g


def _split3(x):
    hi = x.astype(_BF16)
    r1 = x - hi.astype(_F32)
    mid = r1.astype(_BF16)
    lo = (r1 - mid.astype(_F32)).astype(_BF16)
    return hi, mid, lo


def _mask_dot(mask_bf16, x):
    hi, mid, lo = _split3(x)
    d = lambda p: jnp.dot(mask_bf16, p, preferred_element_type=_F32)
    return d(hi) + d(mid) + d(lo)


def _dot_mask(x, mask_bf16):
    hi, mid, lo = _split3(x)
    d = lambda p: jnp.dot(p, mask_bf16, preferred_element_type=_F32)
    return d(hi) + d(mid) + d(lo)


def _in_proj_body(x_ref, g_ref, wqkv_ref, wf_ref, wu_ref, bf_ref,
                  q_ref, k_ref, v_ref, logf_ref, c_ref, u_ref, carry_ref,
                  *, tiles_per_seq, interleave):
    i = pl.program_id(0)
    tm = x_ref.shape[0]
    h = _rms(x_ref[...], g_ref[...]).astype(_BF16)
    for n, o_ref in enumerate((q_ref, k_ref, v_ref)):
        o_ref[...] = jnp.dot(h, wqkv_ref[:, n * ATTN_WIDTH:(n + 1) * ATTN_WIDTH],
                             preferred_element_type=_F32)
    u_ref[...] = jnp.dot(h, wu_ref[...], preferred_element_type=_F32)

    f = jnp.dot(h, wf_ref[...], preferred_element_type=_F32) + bf_ref[...]
    logf = jnp.minimum(f, 0.0) - jnp.log1p(jnp.exp(-jnp.abs(f)))
    logf_ref[...] = logf[:, :N_HEADS]

    kp = -(-tm // LANES) * LANES
    r = lax.broadcasted_iota(jnp.int32, (tm, kp), 0)
    rp = lax.broadcasted_iota(jnp.int32, (tm, kp), 1)
    mask = ((r % interleave) == (rp % interleave)) & ((rp // interleave) <= (r // interleave))
    mask = jnp.where(mask, 1.0, 0.0).astype(_BF16)
    if kp > tm:
        logf_k = jnp.concatenate([logf, jnp.zeros((kp - tm, LANES), _F32)], axis=0)
    else:
        logf_k = logf

    @pl.when(i % tiles_per_seq == 0)
    def _():
        carry_ref[...] = jnp.zeros_like(carry_ref)

    c = _mask_dot(mask, logf_k) + carry_ref[0:1, :]
    c_ref[...] = c[:, :N_HEADS]
    carry_ref[0:1, :] = c[tm - 1:tm, :]


def in_proj(x2d, g, wqkv, wf, wu, bf, *, n_seq, seq_len, tm, interleave, u_time_major):
    m, d = x2d.shape
    assert m == n_seq * seq_len and seq_len % tm == 0 or (n_seq == 1 and m % tm == 0)
    nt = (seq_len if u_time_major else m) // tm
    grid = (m // tm,)
    row = lambda i: (i, 0)
    if u_time_major:
        u_shape = (seq_len, n_seq * ATTN_WIDTH)
        u_map = lambda i: (i % nt, i // nt)
    else:
        u_shape = (m, ATTN_WIDTH)
        u_map = row
    wide = pl.BlockSpec((tm, ATTN_WIDTH), row)
    narrow = pl.BlockSpec((tm, N_HEADS), row)
    body = functools.partial(_in_proj_body, tiles_per_seq=nt, interleave=interleave)
    return pl.pallas_call(
        body,
        grid=grid,
        in_specs=[pl.BlockSpec((tm, d), row), _resident(g.shape), _resident(wqkv.shape),
                  _resident(wf.shape), _resident(wu.shape), _resident(bf.shape)],
        out_specs=[wide, wide, wide, narrow, narrow, pl.BlockSpec((tm, ATTN_WIDTH), u_map)],
        out_shape=[jax.ShapeDtypeStruct((m, ATTN_WIDTH), _F32)] * 3
        + [jax.ShapeDtypeStruct((m, N_HEADS), _F32)] * 2
        + [jax.ShapeDtypeStruct(u_shape, _F32)],
        scratch_shapes=[pltpu.VMEM((SUBLANES, LANES), _F32)],
        compiler_params=_params(("arbitrary",)),
        name="in_proj",
    )(x2d, g, wqkv, wf, wu, bf)


FOX_Q_BLOCK = 256


def _fox_prompt_body(q_ref, k_ref, v_ref, ck_ref, o_ref):
    t_len = q_ref.shape[0]
    scale = 1.0 / math.sqrt(HEAD_DIM)
    for q0 in range(0, t_len, FOX_Q_BLOCK):
        tq = min(FOX_Q_BLOCK, t_len - q0)
        kend = q0 + tq
        q = (q_ref[q0:kend, :] * scale).astype(_BF16)
        k = k_ref[0:kend, :].astype(_BF16)
        s = lax.dot_general(q, k, (((1,), (1,)), ((), ())), preferred_element_type=_F32)
        s = s - ck_ref[0, 0, :, 0:kend]
        row = q0 + lax.broadcasted_iota(jnp.int32, (tq, kend), 0)
        col = lax.broadcasted_iota(jnp.int32, (tq, kend), 1)
        s = jnp.where(col <= row, s, NEG_INF)
        m = jnp.max(s, axis=-1, keepdims=True)
        p = jnp.exp(s - m)
        l = jnp.sum(p, axis=-1, keepdims=True)
        o = jnp.dot(p.astype(_BF16), v_ref[0:kend, :].astype(_BF16), preferred_element_type=_F32)
        o_ref[q0:kend, :] = o / l


def fox_prompt(q, k, v, ck_t, *, n_seq, seq_len):
    m = q.shape[0]
    blk = pl.BlockSpec((seq_len, HEAD_DIM), lambda b, h: (b, h))
    return pl.pallas_call(
        _fox_prompt_body,
        grid=(n_seq, N_HEADS),
        in_specs=[blk, blk, blk, pl.BlockSpec((1, 1, 1, seq_len), lambda b, h: (b, h, 0, 0))],
        out_specs=blk,
        out_shape=jax.ShapeDtypeStruct((m, ATTN_WIDTH), _F32),
        compiler_params=_params(("parallel", "parallel")),
        name="fox_prompt",
    )(q, k, v, ck_t)


BIAS_PAGES_PER_STEP = 16


def _fox_bias_body(pt_ref, *refs, page_size):
    del pt_ref
    page_refs = refs[:BIAS_PAGES_PER_STEP]
    o_ref, x_ref = refs[BIAS_PAGES_PER_STEP:]
    kstep = pl.program_id(2)
    n_pages = x_ref.shape[0]
    width = page_size * N_HEADS
    for i, p_ref in enumerate(page_refs):
        x_ref[pl.ds(kstep * BIAS_PAGES_PER_STEP + i, 1), :] = p_ref[0, 0]

    @pl.when(kstep == pl.num_programs(2) - 1)
    def _():
        x = x_ref[...]
        r = lax.broadcasted_iota(jnp.int32, (width, width), 0)
        c = lax.broadcasted_iota(jnp.int32, (width, width), 1)
        same_head = (r % N_HEADS) == (c % N_HEADS)
        later_key = jnp.where(same_head & (r > c), 1.0, 0.0).astype(_BF16)
        within = _dot_mask(x, later_key)
        rh = lax.broadcasted_iota(jnp.int32, (width, LANES), 0)
        ch = lax.broadcasted_iota(jnp.int32, (width, LANES), 1)
        to_head = jnp.where((rh % N_HEADS) == ch, 1.0, 0.0).astype(_BF16)
        tot = _dot_mask(x, to_head)
        pr = lax.broadcasted_iota(jnp.int32, (n_pages, n_pages), 0)
        pc = lax.broadcasted_iota(jnp.int32, (n_pages, n_pages), 1)
        later_page = jnp.where(pc > pr, 1.0, 0.0).astype(_BF16)
        later = _mask_dot(later_page, tot)
        eh = lax.broadcasted_iota(jnp.int32, (LANES, width), 0)
        ec = lax.broadcasted_iota(jnp.int32, (LANES, width), 1)
        from_head = jnp.where(eh == (ec % N_HEADS), 1.0, 0.0).astype(_BF16)
        o_ref[0, 0] = within + _dot_mask(later, from_head)


def fox_bias(page_table, cache_logf_flat):
    depth, n_pool, _, width = cache_logf_flat.shape
    n_seq, n_pages = page_table.shape
    pps = BIAS_PAGES_PER_STEP
    assert n_pages % pps == 0

    def page_spec(i):
        return pl.BlockSpec((1, 1, 1, width),
                            lambda l, s, k, pt: (l, pt[s, k * pps + i], 0, 0))

    grid_spec = pltpu.PrefetchScalarGridSpec(
        num_scalar_prefetch=1,
        grid=(depth, n_seq, n_pages // pps),
        in_specs=[page_spec(i) for i in range(pps)],
        out_specs=pl.BlockSpec((1, 1, n_pages, width), lambda l, s, k, pt: (l, s, 0, 0)),
        scratch_shapes=[pltpu.VMEM((n_pages, width), _F32)],
    )
    return pl.pallas_call(
        functools.partial(_fox_bias_body, page_size=width // N_HEADS),
        grid_spec=grid_spec,
        out_shape=jax.ShapeDtypeStruct((depth, n_seq, n_pages, width), _F32),
        compiler_params=_params(("arbitrary", "arbitrary", "arbitrary")),
        name="fox_bias",
    )(page_table, *([cache_logf_flat] * pps))


DECODE_PAGES_PER_STEP = 8


def _fox_decode_body(pt_ref, q_ref, kn_ref, vn_ref, cn_ref, bias_ref, *refs, n_new):
    del pt_ref
    pps = DECODE_PAGES_PER_STEP
    k_refs, v_refs = refs[:pps], refs[pps:2 * pps]
    o_ref, m_ref, l_ref, acc_ref = refs[2 * pps:]
    step = pl.program_id(1)
    rows = q_ref.shape[1]
    width = k_refs[0].shape[2]
    contract_last = (((1,), (1,)), ((), ()))
    q = (q_ref[0] * (1.0 / math.sqrt(HEAD_DIM))).astype(_BF16)

    @pl.when(step == 0)
    def _():
        m_ref[...] = jnp.full_like(m_ref, NEG_INF)
        l_ref[...] = jnp.zeros_like(l_ref)
        acc_ref[...] = jnp.zeros_like(acc_ref)

    r = lax.broadcasted_iota(jnp.int32, (rows, width), 0)
    c = lax.broadcasted_iota(jnp.int32, (rows, width), 1)
    same_head = (c % N_HEADS) == (r // n_new)

    logits = []
    for i in range(pps):
        kp = k_refs[i][0, 0].astype(_BF16)
        s = lax.dot_general(q, kp, contract_last, preferred_element_type=_F32)
        s = s + bias_ref[0, 0, i:i + 1, :]
        logits.append(jnp.where(same_head, s, NEG_INF))
    m_old = m_ref[...]
    m_new = m_old
    for s in logits:
        m_new = jnp.maximum(m_new, jnp.max(s, axis=-1, keepdims=True))
    alpha = jnp.exp(m_old - m_new)
    l_new = alpha * l_ref[...]
    acc = alpha * acc_ref[...]
    for i, s in enumerate(logits):
        p = jnp.exp(s - m_new)
        l_new = l_new + jnp.sum(p, axis=-1, keepdims=True)
        acc = acc + jnp.dot(p.astype(_BF16), v_refs[i][0, 0].astype(_BF16),
                            preferred_element_type=_F32)
    m_ref[...] = m_new
    l_ref[...] = l_new
    acc_ref[...] = acc

    @pl.when(step == pl.num_programs(1) - 1)
    def _():
        nk = kn_ref.shape[1]
        kn = kn_ref[0].astype(_BF16)
        s = lax.dot_general(q, kn, contract_last, preferred_element_type=_F32)
        s = s - cn_ref[0]
        rr = lax.broadcasted_iota(jnp.int32, (rows, nk), 0)
        cc = lax.broadcasted_iota(jnp.int32, (rows, nk), 1)
        ok = ((cc % N_HEADS) == (rr // n_new)) & ((cc // N_HEADS) <= (rr % n_new))
        s = jnp.where(ok, s, NEG_INF)
        m_old2 = m_ref[...]
        m_fin = jnp.maximum(m_old2, jnp.max(s, axis=-1, keepdims=True))
        a2 = jnp.exp(m_old2 - m_fin)
        p = jnp.exp(s - m_fin)
        l_fin = a2 * l_ref[...] + jnp.sum(p, axis=-1, keepdims=True)
        acc_fin = a2 * acc_ref[...] + jnp.dot(p.astype(_BF16), vn_ref[0].astype(_BF16),
                                               preferred_element_type=_F32)
        o_ref[0] = acc_fin / l_fin


def fox_decode(page_table, q, k_new, v_new, c_new, bias, cache_k, cache_v, *, layer, n_new):
    n_seq, rows, _ = q.shape
    n_pages = page_table.shape[1]
    width = cache_k.shape[2]
    pps = DECODE_PAGES_PER_STEP
    assert n_pages % pps == 0
    nk = k_new.shape[1]

    def page_spec(i):
        return pl.BlockSpec((1, 1, width, HEAD_DIM),
                            lambda s, k, pt: (layer, pt[s, k * pps + i], 0, 0))

    per_seq = lambda shape: pl.BlockSpec((1,) + shape, lambda s, k, pt: (s, 0, 0))
    grid_spec = pltpu.PrefetchScalarGridSpec(
        num_scalar_prefetch=1,
        grid=(n_seq, n_pages // pps),
        in_specs=[per_seq((rows, HEAD_DIM)), per_seq((nk, HEAD_DIM)), per_seq((nk, HEAD_DIM)),
                  per_seq((1, nk)),
                  pl.BlockSpec((1, 1, pps, width), lambda s, k, pt: (layer, s, k, 0))]
        + [page_spec(i) for i in range(pps)] * 2,
        out_specs=per_seq((rows, HEAD_DIM)),
        scratch_shapes=[pltpu.VMEM((rows, 1), _F32), pltpu.VMEM((rows, 1), _F32),
                        pltpu.VMEM((rows, HEAD_DIM), _F32)],
    )
    return pl.pallas_call(
        functools.partial(_fox_decode_body, n_new=n_new),
        grid_spec=grid_spec,
        out_shape=jax.ShapeDtypeStruct((n_seq, rows, HEAD_DIM), _F32),
        compiler_params=_params(("arbitrary", "arbitrary")),
        name="fox_decode",
    )(page_table, q, k_new, v_new, c_new, bias, *([cache_k] * pps), *([cache_v] * pps))


S5_BLOCKS = 4
S5_BLOCK_IN = 16 * SSM_GROUP
S5_HALF = 16 * SSM_STATE
S5_STATE_COLS = S5_BLOCKS * 2 * S5_HALF
S5_SCAN_PIECE = 256


def _s5_body(u_ref, h0_ref, are_ref, aim_ref, bblk_ref, cblk_ref, d_ref, wglu_ref,
             o_ref, hl_ref, st_ref, *, nb, steps):
    chunk = pl.program_id(0)
    rows = nb * steps
    base = SUBLANES

    @pl.when(chunk == 0)
    def _():
        st_ref[base - nb:base, :] = h0_ref[...]

    u = u_ref[...]
    ub = u.astype(_BF16)
    for j in range(S5_BLOCKS):
        st_ref[base:base + rows, 2 * S5_HALF * j:2 * S5_HALF * (j + 1)] = jnp.dot(
            ub[:, S5_BLOCK_IN * j:S5_BLOCK_IN * (j + 1)], bblk_ref[j], preferred_element_type=_F32)

    sub = SUBLANES // nb
    row_id = lax.broadcasted_iota(jnp.int32, (SUBLANES, S5_SCAN_PIECE), 0)

    def tile_step(t, carry):
        cur = pl.ds(pl.multiple_of(base + t * SUBLANES, SUBLANES), SUBLANES)
        prev = pl.ds(pl.multiple_of(base + (t - 1) * SUBLANES, SUBLANES), SUBLANES)
        for j in range(S5_BLOCKS):
            for piece in range(S5_HALF // S5_SCAN_PIECE):
                lo = 2 * S5_HALF * j + piece * S5_SCAN_PIECE
                cr = slice(lo, lo + S5_SCAN_PIECE)
                ci = slice(lo + S5_HALF, lo + S5_HALF + S5_SCAN_PIECE)
                ar, ai = are_ref[:, cr], aim_ref[:, cr]
                xr, xi = st_ref[cur, cr], st_ref[cur, ci]
                hr, hi = st_ref[prev, cr], st_ref[prev, ci]
                out_r = out_i = None
                for s in range(sub):
                    if nb < SUBLANES:
                        hr, hi = pltpu.roll(hr, nb, axis=0), pltpu.roll(hi, nb, axis=0)
                    hr, hi = ar * hr - ai * hi + xr, ar * hi + ai * hr + xi
                    if s == 0:
                        out_r, out_i = hr, hi
                    else:
                        keep = row_id < s * nb
                        out_r = jnp.where(keep, out_r, hr)
                        out_i = jnp.where(keep, out_i, hi)
                st_ref[cur, cr] = out_r
                st_ref[cur, ci] = out_i
        return carry

    lax.fori_loop(0, rows // SUBLANES, tile_step, 0)

    ys = []
    for j in range(S5_BLOCKS):
        hb = st_ref[base:base + rows, 2 * S5_HALF * j:2 * S5_HALF * (j + 1)].astype(_BF16)
        ys.append(jnp.dot(hb, cblk_ref[j], preferred_element_type=_F32))
    y = jnp.concatenate(ys, axis=-1) + d_ref[...] * u
    z = y * (0.5 * (1.0 + jnp.tanh(math.sqrt(2.0 / math.pi) * (y + 0.044715 * (y * y * y)))))
    gate = jnp.dot(z.astype(_BF16), wglu_ref[...], preferred_element_type=_F32)
    o_ref[...] = z * (1.0 / (1.0 + jnp.exp(-gate)))

    last = st_ref[base + rows - nb:base + rows, :]
    st_ref[base - nb:base, :] = last

    @pl.when(chunk == pl.num_programs(0) - 1)
    def _():
        hl_ref[...] = last


def s5_mixer(u_tb, h0, a_re, a_im, bblk, cblk, d, wglu, *, nb, steps):
    m = u_tb.shape[0]
    rows = nb * steps
    assert m % rows == 0 and rows % SUBLANES == 0
    row = lambda c: (c, 0)
    return pl.pallas_call(
        functools.partial(_s5_body, nb=nb, steps=steps),
        grid=(m // rows,),
        in_specs=[pl.BlockSpec((rows, ATTN_WIDTH), row), _resident(h0.shape),
                  _resident(a_re.shape), _resident(a_im.shape), _resident(bblk.shape),
                  _resident(cblk.shape), _resident(d.shape), _resident(wglu.shape)],
        out_specs=[pl.BlockSpec((rows, ATTN_WIDTH), row),
                   pl.BlockSpec((nb, S5_STATE_COLS), lambda c: (0, 0))],
        out_shape=[jax.ShapeDtypeStruct((m, ATTN_WIDTH), _F32),
                   jax.ShapeDtypeStruct((nb, S5_STATE_COLS), _F32)],
        scratch_shapes=[pltpu.VMEM((SUBLANES + rows, S5_STATE_COLS), _F32)],
        compiler_params=_params(("arbitrary",)),
        name="s5_mixer",
    )(u_tb, h0, a_re, a_im, bblk, cblk, d, wglu)


def _mix_out_body(a_ref, s_ref, x_ref, ga_ref, gs_ref, w_ref, o_ref):
    a = _rms(a_ref[...], ga_ref[...]).astype(_BF16)
    s = _rms(s_ref[...], gs_ref[...]).astype(_BF16)
    y = jnp.dot(a, w_ref[0:ATTN_WIDTH, :], preferred_element_type=_F32)
    y = y + jnp.dot(s, w_ref[ATTN_WIDTH:, :], preferred_element_type=_F32)
    o_ref[...] = x_ref[...] + y


def mix_out(attn, ssm2d, x2d, ga, gs, w, *, tm, tiles_per_seq):
    m, d = x2d.shape
    nt = tiles_per_seq
    row = lambda i: (i, 0)
    return pl.pallas_call(
        _mix_out_body,
        grid=(m // tm,),
        in_specs=[pl.BlockSpec((tm, ATTN_WIDTH), row),
                  pl.BlockSpec((tm, ATTN_WIDTH), lambda i: (i % nt, i // nt)),
                  pl.BlockSpec((tm, d), row), _resident(ga.shape), _resident(gs.shape),
                  _resident(w.shape)],
        out_specs=pl.BlockSpec((tm, d), row),
        out_shape=jax.ShapeDtypeStruct((m, d), _F32),
        compiler_params=_params(("parallel",)),
        name="mix_out",
    )(attn, ssm2d, x2d, ga, gs, w)


def _ffn_body(x_ref, g_ref, wg_ref, wu_ref, wd_ref, o_ref, h_ref):
    f = pl.program_id(1)

    @pl.when(f == 0)
    def _():
        x = x_ref[...]
        h_ref[...] = _rms(x, g_ref[...]).astype(_BF16)
        o_ref[...] = x

    h = h_ref[...]
    gate = jnp.dot(h, wg_ref[...], preferred_element_type=_F32)
    up = jnp.dot(h, wu_ref[...], preferred_element_type=_F32)
    act = (gate * (1.0 / (1.0 + jnp.exp(-gate))) * up).astype(_BF16)
    o_ref[...] += jnp.dot(act, wd_ref[...], preferred_element_type=_F32)


def ffn(x2d, g, wg, wu, wd, *, tm, tf):
    m, d = x2d.shape
    dff = wg.shape[1]
    row = lambda i, f: (i, 0)
    return pl.pallas_call(
        _ffn_body,
        grid=(m // tm, dff // tf),
        in_specs=[pl.BlockSpec((tm, d), row), pl.BlockSpec((1, d), lambda i, f: (0, 0)),
                  pl.BlockSpec((d, tf), lambda i, f: (0, f)),
                  pl.BlockSpec((d, tf), lambda i, f: (0, f)),
                  pl.BlockSpec((tf, d), lambda i, f: (f, 0))],
        out_specs=pl.BlockSpec((tm, d), row),
        out_shape=jax.ShapeDtypeStruct((m, d), _F32),
        scratch_shapes=[pltpu.VMEM((tm, d), _BF16)],
        compiler_params=_params(("parallel", "arbitrary")),
        name="ffn",
    )(x2d, g, wg, wu, wd)


def _final_norm_body(x_ref, g_ref, o_ref):
    o_ref[...] = _rms(x_ref[...], g_ref[...])


def final_norm(x2d, g, *, tm):
    m, d = x2d.shape
    row = lambda i: (i, 0)
    return pl.pallas_call(
        _final_norm_body,
        grid=(m // tm,),
        in_specs=[pl.BlockSpec((tm, d), row), _resident(g.shape)],
        out_specs=pl.BlockSpec((tm, d), row),
        out_shape=jax.ShapeDtypeStruct((m, d), _F32),
        compiler_params=_params(("parallel",)),
        name="final_norm",
    )(x2d, g)


def _s5_params(lam_re, lam_im, log_dt, b_re, b_im, c_re, c_im):
    g, n = lam_re.shape
    lam = lax.complex(lam_re.astype(_F32), lam_im.astype(_F32))
    dt = jnp.exp(log_dt.astype(_F32))[:, None]
    lam_bar = jnp.exp(lam * dt)
    b_bar = ((lam_bar - 1.0) / lam)[..., None] * lax.complex(b_re.astype(_F32), b_im.astype(_F32))
    eye = jnp.eye(16, dtype=_F32)
    gb = g // S5_BLOCKS

    def b_block(x):
        x4 = x.reshape(S5_BLOCKS, gb, n, SSM_GROUP)
        return jnp.einsum('jgnc,gh->jgchn', x4, eye).reshape(S5_BLOCKS, gb * SSM_GROUP, gb * n)

    def c_block(x):
        x4 = x.reshape(S5_BLOCKS, gb, SSM_GROUP, n)
        return jnp.einsum('jgcn,gh->jgnhc', x4, eye).reshape(S5_BLOCKS, gb * n, gb * SSM_GROUP)

    bblk = jnp.concatenate([b_block(jnp.real(b_bar)), b_block(jnp.imag(b_bar))], axis=-1)
    cblk = jnp.concatenate([c_block(c_re.astype(_F32)), -c_block(c_im.astype(_F32))], axis=1)

    def state_row(x):
        x3 = x.reshape(S5_BLOCKS, 1, gb * n)
        return jnp.concatenate([x3, x3], axis=1).reshape(1, S5_STATE_COLS)

    return (state_row(jnp.real(lam_bar)), state_row(jnp.imag(lam_bar)),
            bblk.astype(_BF16), cblk.astype(_BF16))


def _state_to_cols(h_re, h_im):
    nb = h_re.shape[0]
    re = h_re.astype(_F32).reshape(nb, S5_BLOCKS, 1, S5_HALF)
    im = h_im.astype(_F32).reshape(nb, S5_BLOCKS, 1, S5_HALF)
    return jnp.concatenate([re, im], axis=2).reshape(nb, S5_STATE_COLS)


def _cols_to_state(h):
    nb = h.shape[0]
    h4 = h.reshape(nb, S5_BLOCKS, 2, S5_HALF)
    shape = (nb, S5_BLOCKS * 16, SSM_STATE)
    return h4[:, :, 0].reshape(shape), h4[:, :, 1].reshape(shape)


def kernel(x_prompt, x_sample, cache_k, cache_v, cache_logf, state_ssm_re, state_ssm_im, page_table,
           meta_tokens, norm_mix, w_in, b_forget, ssm_lambda_re, ssm_lambda_im, ssm_log_dt,
           ssm_b_re, ssm_b_im, ssm_c_re, ssm_c_im, ssm_d, ssm_w_glu, norm_attn_out, norm_ssm_out,
           w_out, norm_ffn, w_ffn_gate, w_ffn_up, w_ffn_down, norm_final):
    n_prompt, seq, d_model = x_prompt.shape
    n_dec, n_new, _ = x_sample.shape
    depth, n_pool, page_size = cache_k.shape[:3]
    t_len = N_META + seq
    m_p = n_prompt * t_len
    m_s = n_dec * n_new
    d_ff = w_ffn_gate.shape[2]
    assert w_in.shape[2] == 3 * ATTN_WIDTH + N_HEADS + ATTN_WIDTH

    tm_p = _largest_tile(t_len, 384, SUBLANES)
    tm_ffn = _largest_tile(m_p, 768, 16)
    tf = _largest_tile(d_ff, 512, LANES)
    s5_steps_p = _largest_tile(t_len, 384 // n_prompt, 2)
    assert (n_prompt * s5_steps_p) % SUBLANES == 0 and n_dec % SUBLANES == 0

    meta = jnp.broadcast_to(meta_tokens.astype(x_prompt.dtype)[None], (n_prompt, N_META, d_model))
    xp = jnp.concatenate([meta, x_prompt], axis=1).reshape(m_p, d_model)
    xs = jnp.transpose(x_sample, (1, 0, 2)).reshape(m_s, d_model)

    width = page_size * N_HEADS
    cache_k4 = cache_k.reshape(depth, n_pool, width, HEAD_DIM)
    cache_v4 = cache_v.reshape(depth, n_pool, width, HEAD_DIM)
    bias_all = fox_bias(page_table, cache_logf.astype(_F32).reshape(depth, n_pool, 1, width))

    kp, vp, fp, srp, sip = [], [], [], [], []
    kd, vd, fd, srd, sid = [], [], [], [], []
    row2 = lambda x: x.reshape(1, -1).astype(_F32)
    for l in range(depth):
        wl = w_in[l]
        wqkv = wl[:, :3 * ATTN_WIDTH].astype(_BF16)
        wf = jnp.pad(wl[:, 3 * ATTN_WIDTH:3 * ATTN_WIDTH + N_HEADS],
                     ((0, 0), (0, LANES - N_HEADS))).astype(_BF16)
        wu = wl[:, 3 * ATTN_WIDTH + N_HEADS:].astype(_BF16)
        bf = jnp.pad(b_forget[l].astype(_F32), (0, LANES - N_HEADS)).reshape(1, LANES)
        g_mix, g_ffn = row2(norm_mix[l]), row2(norm_ffn[l])
        g_attn, g_ssm = row2(norm_attn_out[l]), row2(norm_ssm_out[l])
        a_re, a_im, bblk, cblk = _s5_params(ssm_lambda_re[l], ssm_lambda_im[l], ssm_log_dt[l],
                                            ssm_b_re[l], ssm_b_im[l], ssm_c_re[l], ssm_c_im[l])
        d_skip = row2(ssm_d[l])
        wglu = ssm_w_glu[l].astype(_BF16)
        wo = w_out[l].astype(_BF16)
        wg, wup, wd = (w_ffn_gate[l].astype(_BF16), w_ffn_up[l].astype(_BF16),
                       w_ffn_down[l].astype(_BF16))

        q, k, v, logf, c, u = in_proj(xp, g_mix, wqkv, wf, wu, bf, n_seq=n_prompt, seq_len=t_len,
                                      tm=tm_p, interleave=1, u_time_major=True)
        ck_t = jnp.transpose(c.reshape(n_prompt, t_len, N_HEADS), (0, 2, 1))
        attn = fox_prompt(q, k, v, ck_t.reshape(n_prompt, N_HEADS, 1, t_len),
                          n_seq=n_prompt, seq_len=t_len)
        zeros = jnp.zeros((n_prompt, S5_STATE_COLS), _F32)
        ssm, h_last = s5_mixer(u.reshape(t_len * n_prompt, ATTN_WIDTH), zeros, a_re, a_im, bblk,
                               cblk, d_skip, wglu, nb=n_prompt, steps=s5_steps_p)
        x1 = mix_out(attn, ssm.reshape(t_len, n_prompt * ATTN_WIDTH), xp, g_attn, g_ssm, wo,
                     tm=tm_p, tiles_per_seq=t_len // tm_p)
        xp = ffn(x1, g_ffn, wg, wup, wd, tm=tm_ffn, tf=tf)
        kp.append(k); vp.append(v); fp.append(logf)
        hr, hi = _cols_to_state(h_last)
        srp.append(hr); sip.append(hi)

        q, k, v, logf, c, u = in_proj(xs, g_mix, wqkv, wf, wu, bf, n_seq=1, seq_len=m_s,
                                      tm=m_s, interleave=n_dec, u_time_major=False)
        heads = lambda x: x.reshape(n_new, n_dec, N_HEADS, HEAD_DIM)
        q_d = jnp.transpose(heads(q), (1, 2, 0, 3)).reshape(n_dec, N_HEADS * n_new, HEAD_DIM)
        k_d = jnp.transpose(heads(k), (1, 0, 2, 3)).reshape(n_dec, n_new * N_HEADS, HEAD_DIM)
        v_d = jnp.transpose(heads(v), (1, 0, 2, 3)).reshape(n_dec, n_new * N_HEADS, HEAD_DIM)
        c_d = jnp.transpose(c.reshape(n_new, n_dec, N_HEADS), (1, 0, 2)).reshape(n_dec, 1, -1)
        o_d = fox_decode(page_table, q_d, k_d, v_d, c_d, bias_all, cache_k4, cache_v4,
                         layer=l, n_new=n_new)
        attn = jnp.transpose(o_d.reshape(n_dec, N_HEADS, n_new, HEAD_DIM),
                             (2, 0, 1, 3)).reshape(m_s, ATTN_WIDTH)
        h0 = _state_to_cols(state_ssm_re[l], state_ssm_im[l])
        ssm, h_last = s5_mixer(u, h0, a_re, a_im, bblk, cblk, d_skip, wglu, nb=n_dec, steps=n_new)
        x1 = mix_out(attn, ssm, xs, g_attn, g_ssm, wo, tm=m_s, tiles_per_seq=1)
        xs = ffn(x1, g_ffn, wg, wup, wd, tm=m_s, tf=tf)
        tb = lambda x, tail: jnp.transpose(x.reshape((n_new, n_dec) + tail),
                                           (1, 0) + tuple(range(2, 2 + len(tail))))
        kd.append(tb(k, (N_HEADS, HEAD_DIM))); vd.append(tb(v, (N_HEADS, HEAD_DIM)))
        fd.append(tb(logf, (N_HEADS,)))
        hr, hi = _cols_to_state(h_last)
        srd.append(hr); sid.append(hi)

    g_fin = row2(norm_final)
    y_prompt = final_norm(xp, g_fin, tm=tm_p).reshape(n_prompt, t_len, d_model)[:, N_META:]
    y_sample = jnp.transpose(final_norm(xs, g_fin, tm=m_s).reshape(n_new, n_dec, d_model), (1, 0, 2))
    shp = lambda x, tail: jnp.stack(x).reshape((depth, n_prompt, t_len) + tail)
    return (y_prompt, y_sample,
            shp(kp, (N_HEADS, HEAD_DIM)), shp(vp, (N_HEADS, HEAD_DIM)), shp(fp, (N_HEADS,)),
            jnp.stack(srp), jnp.stack(sip),
            jnp.stack(kd), jnp.stack(vd), jnp.stack(fd), jnp.stack(srd), jnp.stack(sid))
```
